```python
import math
import jax
import jax.numpy as jnp
from jax import lax
import numpy as np

D_MODEL = 1024
BATCH = 32
SEQ = 2048
DEPTH = 2
DEC_BATCH = 4
DEC_SEQ = 4096
PAST_LEN = 128

N_MIXERS = 2
N_A_LAYERS = (DEPTH + 1) // 2
N_B_LAYERS = DEPTH // 2

HA = 16
Q_LORA = 384
KV_LORA = 256
DN = 64
DR = 32
DV = 64
ROPE_THETA_A = 10000.0

HB = 16
KVB = 8
HD = 128
ROPE_THETA_B = 10000.0
GRID_W = 64

N_EXPERTS = 16
EXPERT_FF = 1024
EC_CAPACITY_FACTOR = 2

Q_BLOCK = 128
RMS_EPS = 1e-6

kernel_name = "hybrid_mla_axialgqa_ec_moe_encoder"


def _rms(x, g):
    xf = x.astype(jnp.float32)
    y = xf * lax.rsqrt(jnp.mean(xf * xf, axis=-1, keepdims=True) + RMS_EPS)
    return y.astype(x.dtype) * g


def _rope(x, pos, theta):
    d = x.shape[-1]
    half = d // 2
    inv = theta ** (-(jnp.arange(half, dtype=jnp.float32) * 2.0 / d))
    ang = pos[:, None] * inv[None, :]
    cos = jnp.cos(ang)[:, None, :]
    sin = jnp.sin(ang)[:, None, :]
    xf = x.astype(jnp.float32)
    x1, x2 = xf[..., :half], xf[..., half:]
    out = jnp.concatenate([x1 * cos - x2 * sin, x1 * sin + x2 * cos], axis=-1)
    return out.astype(x.dtype)


def _axial_rope(x, row, col):
    half = x.shape[-1] // 2
    return jnp.concatenate([_rope(x[..., :half], row, ROPE_THETA_B),
                            _rope(x[..., half:], col, ROPE_THETA_B)], axis=-1)


def _block_attention(q, k, v, scale):
    b, s, h, dk = q.shape
    hk = k.shape[2]
    g = h // hk
    nb = s // Q_BLOCK
    qb = q.reshape(b, nb, Q_BLOCK, hk, g, dk).transpose(1, 0, 2, 3, 4, 5)

    def one_block(qi):
        sc = jnp.einsum('bqkgd,bskd->bkgqs', qi, k,
                        preferred_element_type=jnp.float32) * scale
        p = jax.nn.softmax(sc, axis=-1).astype(v.dtype)
        return jnp.einsum('bkgqs,bskd->bqkgd', p, v)

    out = lax.map(one_block, qb)
    return out.transpose(1, 0, 2, 3, 4, 5).reshape(b, s, h, v.shape[-1])


def _mla(h, w_in, g_q, w_uq, g_kv, w_ukv, w_o, pos):
    b, s, _ = h.shape
    proj = h @ w_in
    c_q = proj[..., :Q_LORA]
    c_kv = proj[..., Q_LORA:Q_LORA + KV_LORA]
    k_r = proj[..., Q_LORA + KV_LORA:]
    q = (_rms(c_q, g_q) @ w_uq).reshape(b, s, HA, DN + DR)
    q = jnp.concatenate([q[..., :DN], _rope(q[..., DN:], pos, ROPE_THETA_A)], axis=-1)
    k_r = _rope(k_r[:, :, None, :], pos, ROPE_THETA_A)
    kv = (_rms(c_kv, g_kv) @ w_ukv).reshape(b, s, HA, DN + DV)
    k = jnp.concatenate([kv[..., :DN], jnp.broadcast_to(k_r, (b, s, HA, DR))], axis=-1)
    v = kv[..., DN:]
    o = _block_attention(q, k, v, (DN + DR) ** -0.5)
    return o.reshape(b, s, HA * DV) @ w_o


def _gqa_axial(h, w_qkv, g_qn, g_kn, w_o, row, col):
    b, s, _ = h.shape
    proj = h @ w_qkv
    q = proj[..., :HB * HD].reshape(b, s, HB, HD)
    k = proj[..., HB * HD:(HB + KVB) * HD].reshape(b, s, KVB, HD)
    v = proj[..., (HB + KVB) * HD:].reshape(b, s, KVB, HD)
    q = _axial_rope(_rms(q, g_qn), row, col)
    k = _axial_rope(_rms(k, g_kn), row, col)
    o = _block_attention(q, k, v, HD ** -0.5)
    return o.reshape(b, s, HB * HD) @ w_o


def _ec_moe(h, w_r, w_g, w_u, w_d):
    b, s, d = h.shape
    n = b * s
    cap = EC_CAPACITY_FACTOR * n // N_EXPERTS
    t = h.reshape(n, d)
    aff = jax.nn.softmax((t @ w_r).astype(jnp.float32), axis=-1)
    gate, idx = lax.top_k(aff.T, cap)
    xe = t[idx]
    hid = jax.nn.silu(jnp.einsum('ecd,edf->ecf', xe, w_g)) * jnp.einsum('ecd,edf->ecf', xe, w_u)
    ye = jnp.einsum('ecf,efd->ecd', hid, w_d) * gate[..., None].astype(t.dtype)
    out = jnp.zeros_like(t).at[idx.reshape(-1)].add(ye.reshape(-1, d))
    return out.reshape(b, s, d)


def _trunk(x, g_attn_a, w_in_a, g_q_a, w_uq_a, g_kv_a, w_ukv_a, w_o_a,
           g_attn_b, w_qkv_b, g_qn_b, g_kn_b, w_o_b,
           g_ffn, w_router, w_gate, w_up, w_down, g_final):
    s = x.shape[1]
    rows = s // GRID_W
    pos = jnp.arange(s, dtype=jnp.float32)
    row = jnp.repeat(jnp.arange(rows, dtype=jnp.float32), GRID_W)
    col = jnp.tile(jnp.arange(GRID_W, dtype=jnp.float32), rows)
    for i in range(DEPTH):
        j = i // N_MIXERS
        if i % N_MIXERS == 0:
            x = x + _mla(_rms(x, g_attn_a[j]), w_in_a[j], g_q_a[j], w_uq_a[j],
                         g_kv_a[j], w_ukv_a[j], w_o_a[j], pos)
        else:
            x = x + _gqa_axial(_rms(x, g_attn_b[j]), w_qkv_b[j], g_qn_b[j], g_kn_b[j],
                               w_o_b[j], row, col)
        x = x + _ec_moe(_rms(x, g_ffn[i]), w_router[i], w_gate[i], w_up[i], w_down[i])
    return _rms(x, g_final)


def setup_inputs(seed: int = 0) -> dict:
    key = jax.random.key(seed)
    ks = jax.random.split(key, 24)

    def nrm(k, shape, scale):
        return jax.random.normal(k, shape, dtype=jnp.float32) * scale

    def gain(k, shape):
        return 1.0 + 0.01 * jax.random.normal(k, shape, dtype=jnp.float32)

    D = D_MODEL
    return {
        "x_prompt": nrm(ks[0], (BATCH, SEQ, D), 1.0),
        "x_sample": nrm(ks[1], (DEC_BATCH, DEC_SEQ, D), 1.0),
        "g_attn_a": gain(ks[2], (N_A_LAYERS, D)),
        "w_in_a": nrm(ks[3], (N_A_LAYERS, D, Q_LORA + KV_LORA + DR), D ** -0.5),
        "g_q_a": gain(ks[4], (N_A_LAYERS, Q_LORA)),
        "w_uq_a": nrm(ks[5], (N_A_LAYERS, Q_LORA, HA * (DN + DR)), Q_LORA ** -0.5),
        "g_kv_a": gain(ks[6], (N_A_LAYERS, KV_LORA)),
        "w_ukv_a": nrm(ks[7], (N_A_LAYERS, KV_LORA, HA * (DN + DV)), KV_LORA ** -0.5),
        "w_o_a": nrm(ks[8], (N_A_LAYERS, HA * DV, D), (HA * DV) ** -0.5),
        "g_attn_b": gain(ks[9], (N_B_LAYERS, D)),
        "w_qkv_b": nrm(ks[10], (N_B_LAYERS, D, (HB + 2 * KVB) * HD), D ** -0.5),
        "g_qn_b": gain(ks[11], (N_B_LAYERS, HD)),
        "g_kn_b": gain(ks[12], (N_B_LAYERS, HD)),
        "w_o_b": nrm(ks[13], (N_B_LAYERS, HB * HD, D), (HB * HD) ** -0.5),
        "g_ffn": gain(ks[14], (DEPTH, D)),
        "w_router": nrm(ks[15], (DEPTH, D, N_EXPERTS), D ** -0.5),
        "w_gate": nrm(ks[16], (DEPTH, N_EXPERTS, D, EXPERT_FF), D ** -0.5),
        "w_up": nrm(ks[17], (DEPTH, N_EXPERTS, D, EXPERT_FF), D ** -0.5),
        "w_down": nrm(ks[18], (DEPTH, N_EXPERTS, EXPERT_FF, D), EXPERT_FF ** -0.5),
        "g_final": gain(ks[19], (D,)),
    }


def reference(x_prompt, x_sample, g_attn_a, w_in_a, g_q_a, w_uq_a, g_kv_a, w_ukv_a, w_o_a,
              g_attn_b, w_qkv_b, g_qn_b, g_kn_b, w_o_b,
              g_ffn, w_router, w_gate, w_up, w_down, g_final):
    y_prompt = _trunk(x_prompt, g_attn_a, w_in_a, g_q_a, w_uq_a, g_kv_a, w_ukv_a, w_o_a,
                      g_attn_b, w_qkv_b, g_qn_b, g_kn_b, w_o_b,
                      g_ffn, w_router, w_gate, w_up, w_down, g_final)
    y_sample = _trunk(x_sample, g_attn_a, w_in_a, g_q_a, w_uq_a, g_kv_a, w_ukv_a, w_o_a,
                      g_attn_b, w_qkv_b, g_qn_b, g_kn_b, w_o_b,
                      g_ffn, w_router, w_gate, w_up, w_down, g_final)
    return (y_prompt, y_sample)
```

```python
import functools

import jax
import jax.numpy as jnp
from jax import lax
from jax.experimental import pallas as pl
from jax.experimental.pallas import tpu as pltpu

D_MODEL = 1024
HA, Q_LORA, KV_LORA, DN, DR, DV = 16, 384, 256, 64, 32, 64
HB, KVB, HD = 16, 8, 128
N_EXPERTS, EXPERT_FF = 16, 1024
EC_CAPACITY_FACTOR = 2
GRID_W = 64
ROPE_THETA = 10000.0
RMS_EPS = 1e-6

LANES = 128
SUBLANES = 8
HEAD_PAD = LANES
EXT = LANES
DE = D_MODEL + EXT
TM = 256
TQ = 256
SEG = 64
WIN = SEG + SUBLANES
FLUSH = 128
ACC_ROWS = FLUSH + WIN
FFN_ROWS = 256
VMEM_LIMIT = 56 * 1024 * 1024

F32 = jnp.float32
BF16 = jnp.bfloat16
I32 = jnp.int32


def _dot(a, b):
    return jnp.dot(a, b, preferred_element_type=F32)


def _dot_nt(a, b):
    return lax.dot_general(a, b, (((1,), (1,)), ((), ())), preferred_element_type=F32)


def _rms(x, g):
    return x * lax.rsqrt(jnp.mean(x * x, axis=-1, keepdims=True) + RMS_EPS) * g


def _params(n_parallel=0, n_arbitrary=1):
    sem = ("parallel",) * n_parallel + ("arbitrary",) * n_arbitrary
    return pltpu.CompilerParams(dimension_semantics=sem, vmem_limit_bytes=VMEM_LIMIT)


def _full(shape):
    zeros = (0,) * len(shape)
    return pl.BlockSpec(shape, lambda *_: zeros)


def _pre_a_kernel(x_ref, g_ref, wcq_ref, wckv_ref, wkr_ref, wkrr_ref, gq_ref, gkv_ref,
                  wuq_ref, wuqr_ref, wuk_ref, wuv_ref, inv_ref, q_ref, k_ref, v_ref,
                  *, tiles_per_seq, scale):
    tm = x_ref.shape[0]
    hb = _rms(x_ref[...], g_ref[...]).astype(BF16)
    cq = _dot(hb, wcq_ref[...])
    ckv = _dot(hb, wckv_ref[...])
    kr = _dot(hb, wkr_ref[...])
    krr = _dot(hb, wkrr_ref[...])
    cqn = _rms(cq, gq_ref[...]).astype(BF16)
    ckvn = _rms(ckv, gkv_ref[...]).astype(BF16)

    s0 = (pl.program_id(0) % tiles_per_seq) * tm
    pos = (s0 + lax.broadcasted_iota(I32, (tm, 1), 0)).astype(F32)
    ang = pos * inv_ref[...]
    cos = jnp.cos(ang)
    sin = jnp.sin(ang)
    cos_h = jnp.concatenate([cos] * HA, axis=1)
    sin_h = jnp.concatenate([sin] * HA, axis=1)

    q = _dot(cqn, wuq_ref[...])
    qr = _dot(cqn, wuqr_ref[...])
    q_ref[...] = ((q * cos_h + qr * sin_h) * scale).astype(BF16)

    k_rope = kr * cos + krr * sin
    k = _dot(ckvn, wuk_ref[...]) + jnp.concatenate([k_rope] * HA, axis=1)
    k_ref[...] = k.astype(BF16)
    v_ref[...] = _dot(ckvn, wuv_ref[...]).astype(BF16)


def _pre_a(x, seq, g_attn, w_in, g_q, w_uq, g_kv, w_ukv):
    n = x.shape[0]
    w_cq = w_in[:, :Q_LORA]
    w_ckv = w_in[:, Q_LORA:Q_LORA + KV_LORA]
    w_kr_raw = w_in[:, Q_LORA + KV_LORA:]
    half = DR // 2

    def rot_half(w):
        return jnp.concatenate([-w[..., half:], w[..., :half]], axis=-1)

    def place_rope(w):
        return jnp.pad(w, ((0, 0), (DN, HEAD_PAD - DN - DR)))

    w_kr = place_rope(w_kr_raw)
    w_krr = place_rope(rot_half(w_kr_raw))
    uq = w_uq.reshape(Q_LORA, HA, DN + DR)
    pad_q = ((0, 0), (0, 0), (0, HEAD_PAD - DN - DR))
    w_uq_p = jnp.pad(uq, pad_q).reshape(Q_LORA, HA * HEAD_PAD)
    uq_rot = jnp.concatenate([jnp.zeros_like(uq[..., :DN]), rot_half(uq[..., DN:])], axis=-1)
    w_uq_r = jnp.pad(uq_rot, pad_q).reshape(Q_LORA, HA * HEAD_PAD)
    ukv = w_ukv.reshape(KV_LORA, HA, DN + DV)
    w_uk = jnp.pad(ukv[..., :DN], ((0, 0), (0, 0), (0, HEAD_PAD - DN))).reshape(KV_LORA, HA * HEAD_PAD)
    w_uv = ukv[..., DN:].reshape(KV_LORA, HA * DV)
    inv = ROPE_THETA ** (-(jnp.arange(half, dtype=F32) * 2.0 / DR))
    inv_l = jnp.pad(jnp.concatenate([inv, inv]), (DN, HEAD_PAD - DN - DR)).reshape(1, HEAD_PAD)

    weights = [w_cq, w_ckv, w_kr, w_krr]
    ups = [w_uq_p, w_uq_r, w_uk, w_uv]
    args = ([x, g_attn.reshape(1, -1)] + [w.astype(BF16) for w in weights]
            + [g_q.reshape(1, -1), g_kv.reshape(1, -1)] + [w.astype(BF16) for w in ups] + [inv_l])
    in_specs = [pl.BlockSpec((TM, D_MODEL), lambda i: (i, 0))] + [_full(a.shape) for a in args[1:]]
    kern = functools.partial(_pre_a_kernel, tiles_per_seq=seq // TM, scale=float((DN + DR) ** -0.5))
    return pl.pallas_call(
        kern, grid=(n // TM,), in_specs=in_specs,
        out_specs=[pl.BlockSpec((TM, HA * HEAD_PAD), lambda i: (i, 0)),
                   pl.BlockSpec((TM, HA * HEAD_PAD), lambda i: (i, 0)),
                   pl.BlockSpec((TM, HA * DV), lambda i: (i, 0))],
        out_shape=[jax.ShapeDtypeStruct((n, HA * HEAD_PAD), BF16),
                   jax.ShapeDtypeStruct((n, HA * HEAD_PAD), BF16),
                   jax.ShapeDtypeStruct((n, HA * DV), BF16)],
        compiler_params=_params(1, 0), name="pre_a")(*args)


def _pre_b_kernel(x_ref, g_ref, wqkv_ref, gq_ref, gk_ref, inv_ref, q_ref, k_ref, v_ref,
                  *, tiles_per_seq, scale):
    tm = x_ref.shape[0]
    hb = _rms(x_ref[...], g_ref[...]).astype(BF16)
    proj = _dot(hb, wqkv_ref[...])

    s0 = (pl.program_id(0) % tiles_per_seq) * tm
    s = s0 + lax.broadcasted_iota(I32, (tm, HD), 0)
    lane = lax.broadcasted_iota(I32, (tm, HD), 1)
    half = HD // 2
    coord = jnp.where(lane < half, s // GRID_W, s % GRID_W).astype(F32)
    ang = coord * inv_ref[...]
    cos = jnp.cos(ang)
    sin = jnp.sin(ang)
    first = (lane % half) < (half // 2)

    def norm_rope(xh, g):
        xn = _rms(xh, g)
        rot = jnp.where(first, -pltpu.roll(xn, HD - half // 2, 1), pltpu.roll(xn, half // 2, 1))
        return xn * cos + rot * sin

    for h in range(HB):
        xh = proj[:, h * HD:(h + 1) * HD]
        q_ref[:, h * HD:(h + 1) * HD] = (norm_rope(xh, gq_ref[...]) * scale).astype(BF16)
    for h in range(KVB):
        xh = proj[:, (HB + h) * HD:(HB + h + 1) * HD]
        k_ref[:, h * HD:(h + 1) * HD] = norm_rope(xh, gk_ref[...]).astype(BF16)
    v_ref[...] = proj[:, (HB + KVB) * HD:].astype(BF16)


def _pre_b(x, seq, g_attn, w_qkv, g_qn, g_kn):
    n = x.shape[0]
    quarter = HD // 4
    inv = ROPE_THETA ** (-(jnp.arange(quarter, dtype=F32) * 2.0 / (HD // 2)))
    inv_l = jnp.concatenate([inv] * 4).reshape(1, HD)
    args = [x, g_attn.reshape(1, -1), w_qkv.astype(BF16), g_qn.reshape(1, -1), g_kn.reshape(1, -1), inv_l]
    in_specs = [pl.BlockSpec((TM, D_MODEL), lambda i: (i, 0))] + [_full(a.shape) for a in args[1:]]
    kern = functools.partial(_pre_b_kernel, tiles_per_seq=seq // TM, scale=float(HD ** -0.5))
    return pl.pallas_call(
        kern, grid=(n // TM,), in_specs=in_specs,
        out_specs=[pl.BlockSpec((TM, HB * HD), lambda i: (i, 0)),
                   pl.BlockSpec((TM, KVB * HD), lambda i: (i, 0)),
                   pl.BlockSpec((TM, KVB * HD), lambda i: (i, 0))],
        out_shape=[jax.ShapeDtypeStruct((n, HB * HD), BF16),
                   jax.ShapeDtypeStruct((n, KVB * HD), BF16),
                   jax.ShapeDtypeStruct((n, KVB * HD), BF16)],
        compiler_params=_params(1, 0), name="pre_b")(*args)


def _attn_kernel(q_ref, k_ref, v_ref, o_ref, *, shared_kv):
    outs = []
    for j in range(2):
        q = q_ref[:, j * HEAD_PAD:(j + 1) * HEAD_PAD]
        if shared_kv:
            k = k_ref[...]
            v = v_ref[...]
        else:
            k = k_ref[:, j * HEAD_PAD:(j + 1) * HEAD_PAD]
            lane = lax.broadcasted_iota(I32, v_ref.shape, 1)
            v = jnp.where((lane >= j * DV) & (lane < (j + 1) * DV), v_ref[...], jnp.zeros_like(v_ref[...]))
        s = _dot_nt(q, k)
        p = jnp.exp(s - jnp.max(s, axis=-1, keepdims=True))
        l = jnp.sum(p, axis=-1, keepdims=True)
        outs.append(_dot(p.astype(BF16), v) / l)
    if shared_kv:
        o_ref[:, :HEAD_PAD] = outs[0].astype(o_ref.dtype)
        o_ref[:, HEAD_PAD:] = outs[1].astype(o_ref.dtype)
    else:
        o_ref[...] = (outs[0] + outs[1]).astype(o_ref.dtype)


def _attention(q, k, v, batch, seq, shared_kv):
    n = q.shape[0]
    n_pairs = q.shape[1] // (2 * HEAD_PAD)
    kw = HEAD_PAD if shared_kv else 2 * HEAD_PAD
    vw = v.shape[1] // n_pairs
    ow = 2 * HEAD_PAD if shared_kv else 2 * DV
    qt = seq // TQ
    return pl.pallas_call(
        functools.partial(_attn_kernel, shared_kv=shared_kv),
        grid=(batch, n_pairs, qt),
        in_specs=[pl.BlockSpec((TQ, 2 * HEAD_PAD), lambda b, h, i: (b * qt + i, h)),
                  pl.BlockSpec((seq, kw), lambda b, h, i: (b, h)),
                  pl.BlockSpec((seq, vw), lambda b, h, i: (b, h))],
        out_specs=pl.BlockSpec((TQ, ow), lambda b, h, i: (b * qt + i, h)),
        out_shape=jax.ShapeDtypeStruct((n, n_pairs * ow), BF16),
        compiler_params=_params(3, 0), name="attn_gqa" if shared_kv else "attn_mla")(q, k, v)


def _post_kernel(o_ref, wo_ref, x_ref, g_ref, wr_ref, wrt_ref, place_ref,
                 x1_ref, hext_ref, afft_ref):
    tm = x_ref.shape[0]
    x1 = x_ref[...] + _dot(o_ref[...], wo_ref[...])
    x1_ref[...] = x1
    hb = _rms(x1, g_ref[...]).astype(BF16)

    lane = lax.broadcasted_iota(I32, (tm, LANES), 1)
    logits = jnp.where(lane < N_EXPERTS, _dot(hb, wr_ref[...]), -1e30)
    ex = jnp.exp(logits - jnp.max(logits, axis=-1, keepdims=True))
    aff = ex / jnp.sum(ex, axis=-1, keepdims=True)
    a_hi = aff.astype(BF16)
    r1 = aff - a_hi.astype(F32)
    a_mid = r1.astype(BF16)
    a_lo = (r1 - a_mid.astype(F32)).astype(BF16)
    ext = _dot(a_hi, place_ref[0]) + _dot(a_mid, place_ref[1]) + _dot(a_lo, place_ref[2])
    t = pl.program_id(0) * tm + lax.broadcasted_iota(I32, (tm, LANES), 0)
    ext = ext + jnp.where(lane == 0, (t >> 8).astype(F32), 0.0) + jnp.where(lane == 1, (t & 255).astype(F32), 0.0)
    hext_ref[:, :D_MODEL] = hb
    hext_ref[:, D_MODEL:] = ext.astype(BF16)

    lt = _dot_nt(wrt_ref[...], hb)
    et = jnp.exp(lt - jnp.max(lt, axis=0, keepdims=True))
    at = et / jnp.sum(et, axis=0, keepdims=True)
    for c in range(tm // LANES):
        afft_ref[c] = at[:, c * LANES:(c + 1) * LANES]


def _post(o, w_o, x, g_ffn, w_r):
    n = x.shape[0]
    ko = o.shape[1]
    wr_p = jnp.pad(w_r, ((0, 0), (0, LANES - N_EXPERTS))).astype(BF16)
    wr_t = w_r.T.astype(BF16)
    e = jnp.arange(N_EXPERTS)
    place = jnp.stack([jnp.zeros((LANES, EXT), F32).at[e, 2 + 3 * e + j].set(1.0) for j in range(3)]).astype(BF16)
    args = [o, w_o.astype(BF16), x, g_ffn.reshape(1, -1), wr_p, wr_t, place]
    in_specs = [pl.BlockSpec((TM, ko), lambda i: (i, 0)), _full(args[1].shape),
                pl.BlockSpec((TM, D_MODEL), lambda i: (i, 0))] + [_full(a.shape) for a in args[3:]]
    return pl.pallas_call(
        _post_kernel, grid=(n // TM,), in_specs=in_specs,
        out_specs=[pl.BlockSpec((TM, D_MODEL), lambda i: (i, 0)),
                   pl.BlockSpec((TM, DE), lambda i: (i, 0)),
                   pl.BlockSpec((TM // LANES, N_EXPERTS, LANES), lambda i: (i, 0, 0))],
        out_shape=[jax.ShapeDtypeStruct((n, D_MODEL), F32),
                   jax.ShapeDtypeStruct((n, DE), BF16),
                   jax.ShapeDtypeStruct((n // LANES, N_EXPERTS, LANES), F32)],
        compiler_params=_params(1, 0), name="post")(*args)


def _select_kernel(aff_ref, tri_ref, pos_ref, carry_ref, *, cap):
    nblk = aff_ref.shape[0]
    bits = lax.bitcast_convert_type(aff_ref[...], I32)

    def count(mask):
        c = jnp.sum(mask.astype(F32), axis=0, keepdims=True)
        return jnp.sum(c, axis=2, keepdims=True)

    def search(_, lohi):
        lo, hi = lohi
        mid = lo + ((hi - lo) >> 1)
        ok = count(bits >= mid) >= cap
        return jnp.where(ok, mid, lo), jnp.where(ok, hi, mid)

    lo0 = jnp.zeros((1, N_EXPERTS, 1), I32)
    hi0 = jnp.full((1, N_EXPERTS, 1), 0x7F800001, I32)
    thr, _ = lax.fori_loop(0, 31, search, (lo0, hi0))

    def prefix(mask):
        m = mask.astype(F32)
        loc = _dot(m.astype(BF16).reshape(nblk * N_EXPERTS, LANES), tri_ref[...])
        loc = loc.reshape(nblk, N_EXPERTS, LANES)
        bsum = loc[:, :, LANES - 1:]
        c = bsum
        k = 1
        while k < nblk:
            c = c + jnp.concatenate([jnp.zeros((k, N_EXPERTS, 1), F32), c[:nblk - k]], axis=0)
            k *= 2
        carry = c - bsum
        return carry + loc - m, carry

    gt = bits > thr
    tie = bits == thr
    need = cap - count(gt)
    tie_rank, _ = prefix(tie)
    sel = gt | (tie & (tie_rank < need))
    pos, carry = prefix(sel)
    pos_ref[...] = jnp.where(sel, pos, -1.0).astype(I32)
    carry_ref[...] = jnp.broadcast_to(carry, carry_ref.shape).astype(I32)


def _select(afft, cap):
    tri = (jnp.arange(LANES)[:, None] <= jnp.arange(LANES)[None, :]).astype(BF16)
    return pl.pallas_call(
        functools.partial(_select_kernel, cap=cap), grid=(1,),
        in_specs=[_full(afft.shape), _full(tri.shape)],
        out_specs=[_full(afft.shape), _full(afft.shape)],
        out_shape=[jax.ShapeDtypeStruct(afft.shape, I32), jax.ShapeDtypeStruct(afft.shape, I32)],
        compiler_params=_params(0, 1), name="select")(afft, tri)


def _tile_counts(base_ref, i):
    bases = [base_ref[i * N_EXPERTS + e] for e in range(N_EXPERTS)]
    cnts = [base_ref[(i + 1) * N_EXPERTS + e] - bases[e] for e in range(N_EXPERTS)]
    most = cnts[0]
    for c in cnts[1:]:
        most = jnp.maximum(most, c)
    return bases, cnts, (most + SEG - 1) // SEG


def _scatter_kernel(base_ref, pos_ref, h_ref, xe_ref, acc_ref, sem):
    i = pl.program_id(0)

    @pl.when(i == 0)
    def _():
        acc_ref[...] = jnp.zeros_like(acc_ref)

    bases, cnts, nrounds = _tile_counts(base_ref, i)
    hb = h_ref[...]
    riota = lax.broadcasted_iota(I32, (WIN, TM), 0)

    def one_round(r, carry):
        onehots, done = [], []
        for e in range(N_EXPERTS):
            s0 = bases[e] + jnp.minimum(r * SEG, cnts[e])
            p = jnp.concatenate([pos_ref[c, e:e + 1, :] for c in range(TM // LANES)], axis=1)
            rank = p - bases[e] - r * SEG
            valid = (p >= 0) & (rank >= 0) & (rank < SEG)
            onehots.append(((riota == rank + (s0 & (SUBLANES - 1))) & valid).astype(BF16))
            done.append(s0)
        rows = _dot(jnp.concatenate(onehots, axis=0), hb)
        for e in range(N_EXPERTS):
            s0 = done[e]
            fill = s0 & (FLUSH - 1)
            al = pl.multiple_of(fill - (fill & (SUBLANES - 1)), SUBLANES)
            acc_ref[e, pl.ds(al, WIN), :] += rows[e * WIN:(e + 1) * WIN]
            s1 = bases[e] + jnp.minimum((r + 1) * SEG, cnts[e])

            @pl.when((s1 // FLUSH) > (s0 // FLUSH))
            def _(e=e, s0=s0):
                dst = pl.multiple_of((s0 // FLUSH) * FLUSH, FLUSH)
                cp = pltpu.make_async_copy(acc_ref.at[e, pl.ds(0, FLUSH)],
                                           xe_ref.at[e, pl.ds(dst, FLUSH)], sem)
                cp.start()
                cp.wait()
                acc_ref[e, 0:WIN, :] = acc_ref[e, FLUSH:ACC_ROWS, :]
                acc_ref[e, WIN:ACC_ROWS, :] = jnp.zeros((ACC_ROWS - WIN, DE), F32)
        return carry

    lax.fori_loop(0, nrounds, one_round, 0)


def _scatter(base, pos, hext, cap):
    n = hext.shape[0]
    grid_spec = pltpu.PrefetchScalarGridSpec(
        num_scalar_prefetch=1, grid=(n // TM,),
        in_specs=[pl.BlockSpec((TM // LANES, N_EXPERTS, LANES), lambda i, b: (i, 0, 0)),
                  pl.BlockSpec((TM, DE), lambda i, b: (i, 0))],
        out_specs=pl.BlockSpec(memory_space=pl.ANY),
        scratch_shapes=[pltpu.VMEM((N_EXPERTS, ACC_ROWS, DE), F32), pltpu.SemaphoreType.DMA(())])
    return pl.pallas_call(
        _scatter_kernel, grid_spec=grid_spec,
        out_shape=jax.ShapeDtypeStruct((N_EXPERTS, cap, DE), F32),
        compiler_params=_params(0, 1), name="scatter")(base, pos, hext)


def _ffn_kernel(xe_ref, wg_ref, wu_ref, wd_ref, ye_ref):
    e = pl.program_id(0)
    rows = xe_ref.shape[0]
    x = xe_ref[:, :D_MODEL].astype(BF16)
    ext = xe_ref[:, D_MODEL:]
    lane = lax.broadcasted_iota(I32, (rows, EXT), 1)
    mine = (lane >= 2 + 3 * e) & (lane < 5 + 3 * e)
    gate = jnp.sum(jnp.where(mine, ext, 0.0), axis=-1, keepdims=True)
    h1 = _dot(x, wg_ref[...])
    hid = h1 * jax.nn.sigmoid(h1) * _dot(x, wu_ref[...])
    ye_ref[:, :D_MODEL] = _dot(hid.astype(BF16), wd_ref[...]) * gate
    ye_ref[:, D_MODEL:] = ext


def _ffn(xe, w_g, w_u, w_d):
    _, cap, _ = xe.shape
    rows = min(FFN_ROWS, cap)
    wspec = lambda shape: pl.BlockSpec((None,) + shape, lambda e, c: (e, 0, 0))
    return pl.pallas_call(
        _ffn_kernel, grid=(N_EXPERTS, cap // rows),
        in_specs=[pl.BlockSpec((None, rows, DE), lambda e, c: (e, c, 0)),
                  wspec((D_MODEL, EXPERT_FF)), wspec((D_MODEL, EXPERT_FF)), wspec((EXPERT_FF, D_MODEL))],
        out_specs=pl.BlockSpec((None, rows, DE), lambda e, c: (e, c, 0)),
        out_shape=jax.ShapeDtypeStruct(xe.shape, F32),
        compiler_params=_params(2, 0), name="ffn")(xe, w_g, w_u, w_d)


def _combine_kernel(base_ref, x1_ref, ye_ref, g_ref, out_ref, ybuf_ref, sem, *, cap, final):
    i = pl.program_id(0)
    bases, cnts, nrounds = _tile_counts(base_ref, i)
    out_ref[...] = x1_ref[...]
    lane = lax.broadcasted_iota(I32, (1, N_EXPERTS * WIN), 1)
    trow = (i * TM + lax.broadcasted_iota(I32, (TM, 1), 0)).astype(F32)

    def one_round(r, carry):
        copies, starts = [], []
        for e in range(N_EXPERTS):
            lo = bases[e] + r * SEG
            st = pl.multiple_of(jnp.minimum(lo - (lo & (SUBLANES - 1)), cap - WIN), SUBLANES)
            cp = pltpu.make_async_copy(ye_ref.at[e, pl.ds(st, WIN)], ybuf_ref.at[e], sem)
            cp.start()
            copies.append(cp)
            starts.append(st)
        for cp in copies:
            cp.wait()
        y = ybuf_ref[...].reshape(N_EXPERTS * WIN, DE)
        ext_t = y[:, D_MODEL:].T
        tid = ext_t[0:1, :] * 256.0 + ext_t[1:2, :]
        slot = jnp.zeros_like(lane)
        lo_v = jnp.zeros_like(lane)
        hi_v = jnp.zeros_like(lane)
        for e in range(N_EXPERTS):
            seg = (lane >= e * WIN) & (lane < (e + 1) * WIN)
            slot = jnp.where(seg, starts[e] + lane - e * WIN, slot)
            lo_v = jnp.where(seg, bases[e] + r * SEG, lo_v)
            hi_v = jnp.where(seg, bases[e] + jnp.minimum((r + 1) * SEG, cnts[e]), hi_v)
        valid = (slot >= lo_v) & (slot < hi_v)
        onehot = ((tid == trow) & valid).astype(BF16)
        out_ref[...] += _dot(onehot, y[:, :D_MODEL].astype(BF16))
        return carry

    lax.fori_loop(0, nrounds, one_round, 0)
    if final:
        out_ref[...] = _rms(out_ref[...], g_ref[...])


def _combine(base, x1, ye, g_final, final):
    n = x1.shape[0]
    cap = ye.shape[1]
    grid_spec = pltpu.PrefetchScalarGridSpec(
        num_scalar_prefetch=1, grid=(n // TM,),
        in_specs=[pl.BlockSpec((TM, D_MODEL), lambda i, b: (i, 0)),
                  pl.BlockSpec(memory_space=pl.ANY),
                  pl.BlockSpec((1, D_MODEL), lambda i, b: (0, 0))],
        out_specs=pl.BlockSpec((TM, D_MODEL), lambda i, b: (i, 0)),
        scratch_shapes=[pltpu.VMEM((N_EXPERTS, WIN, DE), F32), pltpu.SemaphoreType.DMA(())])
    return pl.pallas_call(
        functools.partial(_combine_kernel, cap=cap, final=final), grid_spec=grid_spec,
        out_shape=jax.ShapeDtypeStruct((n, D_MODEL), F32),
        compiler_params=_params(1, 0), name="combine")(base, x1, ye, g_final.reshape(1, -1))


def _moe(x1, hext, afft, w_g, w_u, w_d, g_final, final):
    n = x1.shape[0]
    cap = EC_CAPACITY_FACTOR * n // N_EXPERTS
    pos, carry = _select(afft, cap)
    per_tile = carry[::TM // LANES, :, 0]
    base = jnp.concatenate([per_tile, jnp.full((1, N_EXPERTS), cap, I32)], axis=0).reshape(-1)
    xe = _scatter(base, pos, hext, cap)
    ye = _ffn(xe, w_g, w_u, w_d)
    return _combine(base, x1, ye, g_final, final)


def _trunk(x, w):
    batch, seq, _ = x.shape
    x0 = x.reshape(batch * seq, D_MODEL)
    q, k, v = _pre_a(x0, seq, w["g_attn_a"][0], w["w_in_a"][0], w["g_q_a"][0], w["w_uq_a"][0],
                     w["g_kv_a"][0], w["w_ukv_a"][0])
    o = _attention(q, k, v, batch, seq, shared_kv=False)
    x1, hext, afft = _post(o, w["w_o_a"][0], x0, w["g_ffn"][0], w["w_router"][0])
    x2 = _moe(x1, hext, afft, w["w_gate"][0], w["w_up"][0], w["w_down"][0], w["g_final"], final=False)
    q, k, v = _pre_b(x2, seq, w["g_attn_b"][0], w["w_qkv_b"][0], w["g_qn_b"][0], w["g_kn_b"][0])
    o = _attention(q, k, v, batch, seq, shared_kv=True)
    x3, hext, afft = _post(o, w["w_o_b"][0], x2, w["g_ffn"][1], w["w_router"][1])
    y = _moe(x3, hext, afft, w["w_gate"][1], w["w_up"][1], w["w_down"][1], w["g_final"], final=True)
    return y.reshape(batch, seq, D_MODEL)


def kernel(x_prompt, x_sample, g_attn_a, w_in_a, g_q_a, w_uq_a, g_kv_a, w_ukv_a, w_o_a, g_attn_b, w_qkv_b, g_qn_b, g_kn_b, w_o_b, g_ffn, w_router, w_gate, w_up, w_down, g_final):
    w = dict(g_attn_a=g_attn_a, w_in_a=w_in_a, g_q_a=g_q_a, w_uq_a=w_uq_a, g_kv_a=g_kv_a,
             w_ukv_a=w_ukv_a, w_o_a=w_o_a, g_attn_b=g_attn_b, w_qkv_b=w_qkv_b, g_qn_b=g_qn_b,
             g_kn_b=g_kn_b, w_o_b=w_o_b, g_ffn=g_ffn, w_router=w_router,
             w_gate=w_gate.astype(BF16), w_up=w_up.astype(BF16), w_down=w_down.astype(BF16),
             g_final=g_final)
    return _trunk(x_prompt, w), _trunk(x_sample, w)
```

```python
import functools

import jax
import jax.numpy as jnp
from jax import lax
from jax.experimental import pallas as pl
from jax.experimental.pallas import tpu as pltpu

D_MODEL = 1024
HA, Q_LORA, KV_LORA, DN, DR, DV = 16, 384, 256, 64, 32, 64
HB, KVB, HD = 16, 8, 128
N_EXPERTS, EXPERT_FF = 16, 1024
EC_CAPACITY_FACTOR = 2
GRID_W = 64
ROPE_THETA = 10000.0
RMS_EPS = 1e-6

LANES = 128
SUBLANES = 8
HEAD_PAD = LANES
EXT = LANES
DE = D_MODEL + EXT
TM = 256
TQ = 512
SEG = 48
WIN = SEG + SUBLANES
FLUSH = 128
ACC_ROWS = FLUSH + WIN
FLUSH_BUFS = 4
LOG2E = 1.4426950408889634
FFN_ROWS = 256
VMEM_LIMIT = 56 * 1024 * 1024

F32 = jnp.float32
BF16 = jnp.bfloat16
I32 = jnp.int32


def _dot(a, b):
    return jnp.dot(a, b, preferred_element_type=F32)


def _dot_nt(a, b):
    return lax.dot_general(a, b, (((1,), (1,)), ((), ())), preferred_element_type=F32)


def _rms(x, g):
    return x * lax.rsqrt(jnp.mean(x * x, axis=-1, keepdims=True) + RMS_EPS) * g


def _params(n_parallel=0, n_arbitrary=1):
    sem = ("parallel",) * n_parallel + ("arbitrary",) * n_arbitrary
    return pltpu.CompilerParams(dimension_semantics=sem, vmem_limit_bytes=VMEM_LIMIT)


def _full(shape):
    zeros = (0,) * len(shape)
    return pl.BlockSpec(shape, lambda *_: zeros)


def _pre_a_kernel(x_ref, g_ref, wcq_ref, wckv_ref, wkr_ref, wkrr_ref, gq_ref, gkv_ref,
                  wuq_ref, wuqr_ref, wuk_ref, wuv_ref, inv_ref, q_ref, k_ref, v_ref,
                  *, tiles_per_seq, scale):
    tm = x_ref.shape[0]
    hb = _rms(x_ref[...], g_ref[...]).astype(BF16)
    cq = _dot(hb, wcq_ref[...])
    ckv = _dot(hb, wckv_ref[...])
    kr = _dot(hb, wkr_ref[...])
    krr = _dot(hb, wkrr_ref[...])
    cqn = _rms(cq, gq_ref[...]).astype(BF16)
    ckvn = _rms(ckv, gkv_ref[...]).astype(BF16)

    s0 = (pl.program_id(0) % tiles_per_seq) * tm
    pos = (s0 + lax.broadcasted_iota(I32, (tm, 1), 0)).astype(F32)
    ang = pos * inv_ref[...]
    cos = jnp.cos(ang)
    sin = jnp.sin(ang)
    cos_h = jnp.concatenate([cos] * HA, axis=1)
    sin_h = jnp.concatenate([sin] * HA, axis=1)

    q = _dot(cqn, wuq_ref[...])
    qr = _dot(cqn, wuqr_ref[...])
    q_ref[...] = ((q * cos_h + qr * sin_h) * scale).astype(BF16)

    k_rope = kr * cos + krr * sin
    k = _dot(ckvn, wuk_ref[...]) + jnp.concatenate([k_rope] * HA, axis=1)
    k_ref[...] = k.astype(BF16)
    v_ref[...] = _dot(ckvn, wuv_ref[...]).astype(BF16)


def _pre_a(x, seq, g_attn, w_in, g_q, w_uq, g_kv, w_ukv):
    n = x.shape[0]
    w_cq = w_in[:, :Q_LORA]
    w_ckv = w_in[:, Q_LORA:Q_LORA + KV_LORA]
    w_kr_raw = w_in[:, Q_LORA + KV_LORA:]
    half = DR // 2

    def rot_half(w):
        return jnp.concatenate([-w[..., half:], w[..., :half]], axis=-1)

    def place_rope(w):
        return jnp.pad(w, ((0, 0), (DN, HEAD_PAD - DN - DR)))

    w_kr = place_rope(w_kr_raw)
    w_krr = place_rope(rot_half(w_kr_raw))
    uq = w_uq.reshape(Q_LORA, HA, DN + DR)
    pad_q = ((0, 0), (0, 0), (0, HEAD_PAD - DN - DR))
    w_uq_p = jnp.pad(uq, pad_q).reshape(Q_LORA, HA * HEAD_PAD)
    uq_rot = jnp.concatenate([jnp.zeros_like(uq[..., :DN]), rot_half(uq[..., DN:])], axis=-1)
    w_uq_r = jnp.pad(uq_rot, pad_q).reshape(Q_LORA, HA * HEAD_PAD)
    ukv = w_ukv.reshape(KV_LORA, HA, DN + DV)
    w_uk = jnp.pad(ukv[..., :DN], ((0, 0), (0, 0), (0, HEAD_PAD - DN))).reshape(KV_LORA, HA * HEAD_PAD)
    w_uv = ukv[..., DN:].reshape(KV_LORA, HA * DV)
    inv = ROPE_THETA ** (-(jnp.arange(half, dtype=F32) * 2.0 / DR))
    inv_l = jnp.pad(jnp.concatenate([inv, inv]), (DN, HEAD_PAD - DN - DR)).reshape(1, HEAD_PAD)

    weights = [w_cq, w_ckv, w_kr, w_krr]
    ups = [w_uq_p, w_uq_r, w_uk, w_uv]
    args = ([x, g_attn.reshape(1, -1)] + [w.astype(BF16) for w in weights]
            + [g_q.reshape(1, -1), g_kv.reshape(1, -1)] + [w.astype(BF16) for w in ups] + [inv_l])
    in_specs = [pl.BlockSpec((TM, D_MODEL), lambda i: (i, 0))] + [_full(a.shape) for a in args[1:]]
    kern = functools.partial(_pre_a_kernel, tiles_per_seq=seq // TM, scale=float((DN + DR) ** -0.5 * LOG2E))
    return pl.pallas_call(
        kern, grid=(n // TM,), in_specs=in_specs,
        out_specs=[pl.BlockSpec((TM, HA * HEAD_PAD), lambda i: (i, 0)),
                   pl.BlockSpec((TM, HA * HEAD_PAD), lambda i: (i, 0)),
                   pl.BlockSpec((TM, HA * DV), lambda i: (i, 0))],
        out_shape=[jax.ShapeDtypeStruct((n, HA * HEAD_PAD), BF16),
                   jax.ShapeDtypeStruct((n, HA * HEAD_PAD), BF16),
                   jax.ShapeDtypeStruct((n, HA * DV), BF16)],
        compiler_params=_params(1, 0), name="pre_a")(*args)


def _pre_b_kernel(x_ref, g_ref, wqkv_ref, gq_ref, gk_ref, inv_ref, q_ref, k_ref, v_ref,
                  *, tiles_per_seq, scale):
    tm = x_ref.shape[0]
    hb = _rms(x_ref[...], g_ref[...]).astype(BF16)
    proj = _dot(hb, wqkv_ref[...])

    s0 = (pl.program_id(0) % tiles_per_seq) * tm
    s = s0 + lax.broadcasted_iota(I32, (tm, HD), 0)
    lane = lax.broadcasted_iota(I32, (tm, HD), 1)
    half = HD // 2
    coord = jnp.where((lane % half) < (half // 2), s // GRID_W, s % GRID_W).astype(F32)
    ang = coord * inv_ref[...]
    cos = jnp.cos(ang)
    sin = jnp.where(lane < half, -jnp.sin(ang), jnp.sin(ang))

    def norm_rope(xh, g):
        xn = _rms(xh, g)
        return xn * cos + pltpu.roll(xn, half, 1) * sin

    for h in range(HB):
        xh = proj[:, h * HD:(h + 1) * HD]
        q_ref[:, h * HD:(h + 1) * HD] = (norm_rope(xh, gq_ref[...]) * scale).astype(BF16)
    for h in range(KVB):
        xh = proj[:, (HB + h) * HD:(HB + h + 1) * HD]
        k_ref[:, h * HD:(h + 1) * HD] = norm_rope(xh, gk_ref[...]).astype(BF16)
    v_ref[...] = proj[:, (HB + KVB) * HD:].astype(BF16)


def _pre_b(x, seq, g_attn, w_qkv, g_qn, g_kn):
    n = x.shape[0]
    quarter = HD // 4
    inv = ROPE_THETA ** (-(jnp.arange(quarter, dtype=F32) * 2.0 / (HD // 2)))
    inv_l = jnp.concatenate([inv] * 4).reshape(1, HD)
    perm = jnp.concatenate([jnp.arange(0, quarter), jnp.arange(2 * quarter, 3 * quarter),
                            jnp.arange(quarter, 2 * quarter), jnp.arange(3 * quarter, HD)])
    qk = w_qkv[:, :(HB + KVB) * HD].reshape(D_MODEL, HB + KVB, HD)[:, :, perm]
    w_perm = jnp.concatenate([qk.reshape(D_MODEL, (HB + KVB) * HD), w_qkv[:, (HB + KVB) * HD:]], axis=1)
    args = [x, g_attn.reshape(1, -1), w_perm.astype(BF16), g_qn[perm].reshape(1, -1),
            g_kn[perm].reshape(1, -1), inv_l]
    in_specs = [pl.BlockSpec((TM, D_MODEL), lambda i: (i, 0))] + [_full(a.shape) for a in args[1:]]
    kern = functools.partial(_pre_b_kernel, tiles_per_seq=seq // TM, scale=float(HD ** -0.5 * LOG2E))
    return pl.pallas_call(
        kern, grid=(n // TM,), in_specs=in_specs,
        out_specs=[pl.BlockSpec((TM, HB * HD), lambda i: (i, 0)),
                   pl.BlockSpec((TM, KVB * HD), lambda i: (i, 0)),
                   pl.BlockSpec((TM, KVB * HD), lambda i: (i, 0))],
        out_shape=[jax.ShapeDtypeStruct((n, HB * HD), BF16),
                   jax.ShapeDtypeStruct((n, KVB * HD), BF16),
                   jax.ShapeDtypeStruct((n, KVB * HD), BF16)],
        compiler_params=_params(1, 0), name="pre_b")(*args)


def _attn_kernel(q_ref, k_ref, v_ref, o_ref, vaug_ref, *, shared_kv):
    seq = v_ref.shape[0]

    @pl.when(pl.program_id(2) == 0)
    def _():
        v = v_ref[...]
        if shared_kv:
            lane = lax.broadcasted_iota(I32, (seq, HEAD_PAD), 1)
            vaug_ref[0, :, :HEAD_PAD] = v
            vaug_ref[0, :, HEAD_PAD:] = jnp.where(lane == 0, 1.0, 0.0).astype(BF16)
        else:
            lane = lax.broadcasted_iota(I32, v.shape, 1)
            vaug_ref[0] = jnp.where(lane < DV, v, jnp.where(lane == DV, 1.0, 0.0).astype(BF16))
            vaug_ref[1] = jnp.where(lane >= DV, v, jnp.where(lane == 0, 1.0, 0.0).astype(BF16))

    outs = []
    for j in range(2):
        q = q_ref[:, j * HEAD_PAD:(j + 1) * HEAD_PAD]
        k = k_ref[...] if shared_kv else k_ref[:, j * HEAD_PAD:(j + 1) * HEAD_PAD]
        s = _dot_nt(q, k)
        p = jnp.exp2(s - jnp.max(s, axis=-1, keepdims=True)).astype(BF16)
        outs.append(_dot(p, vaug_ref[0 if shared_kv else j]))
    if shared_kv:
        for j in range(2):
            o = outs[j][:, :HEAD_PAD] / outs[j][:, HEAD_PAD:HEAD_PAD + 1]
            o_ref[:, j * HEAD_PAD:(j + 1) * HEAD_PAD] = o.astype(o_ref.dtype)
    else:
        lane = lax.broadcasted_iota(I32, outs[0].shape, 1)
        o = jnp.where(lane < DV, outs[0] / outs[0][:, DV:DV + 1], outs[1] / outs[1][:, 0:1])
        o_ref[...] = o.astype(o_ref.dtype)


def _attention(q, k, v, batch, seq, shared_kv):
    n = q.shape[0]
    n_pairs = q.shape[1] // (2 * HEAD_PAD)
    kw = HEAD_PAD if shared_kv else 2 * HEAD_PAD
    vw = v.shape[1] // n_pairs
    ow = 2 * HEAD_PAD if shared_kv else 2 * DV
    tq = min(TQ, seq)
    qt = seq // tq
    vaug = (1, seq, 2 * HEAD_PAD) if shared_kv else (2, seq, vw)
    return pl.pallas_call(
        functools.partial(_attn_kernel, shared_kv=shared_kv),
        grid=(batch, n_pairs, qt),
        in_specs=[pl.BlockSpec((tq, 2 * HEAD_PAD), lambda b, h, i: (b * qt + i, h)),
                  pl.BlockSpec((seq, kw), lambda b, h, i: (b, h)),
                  pl.BlockSpec((seq, vw), lambda b, h, i: (b, h))],
        out_specs=pl.BlockSpec((tq, ow), lambda b, h, i: (b * qt + i, h)),
        out_shape=jax.ShapeDtypeStruct((n, n_pairs * ow), BF16),
        scratch_shapes=[pltpu.VMEM(vaug, BF16)],
        compiler_params=_params(2, 1), name="attn_gqa" if shared_kv else "attn_mla")(q, k, v)


def _post_kernel(o_ref, wo_ref, x_ref, g_ref, wr_ref, wrt_ref, place_ref,
                 x1_ref, hext_ref, afft_ref):
    tm = x_ref.shape[0]
    x1 = x_ref[...] + _dot(o_ref[...], wo_ref[...])
    x1_ref[...] = x1
    hb = _rms(x1, g_ref[...]).astype(BF16)

    lane = lax.broadcasted_iota(I32, (tm, LANES), 1)
    logits = jnp.where(lane < N_EXPERTS, _dot(hb, wr_ref[...]), -1e30)
    ex = jnp.exp(logits - jnp.max(logits, axis=-1, keepdims=True))
    aff = ex / jnp.sum(ex, axis=-1, keepdims=True)
    a_hi = aff.astype(BF16)
    r1 = aff - a_hi.astype(F32)
    a_mid = r1.astype(BF16)
    a_lo = (r1 - a_mid.astype(F32)).astype(BF16)
    ext = _dot(a_hi, place_ref[0]) + _dot(a_mid, place_ref[1]) + _dot(a_lo, place_ref[2])
    t = pl.program_id(0) * tm + lax.broadcasted_iota(I32, (tm, LANES), 0)
    ext = ext + jnp.where(lane == 0, (t >> 8).astype(F32), 0.0) + jnp.where(lane == 1, (t & 255).astype(F32), 0.0)
    hext_ref[:, :D_MODEL] = hb
    hext_ref[:, D_MODEL:] = ext.astype(BF16)

    lt = _dot_nt(wrt_ref[...], hb)
    et = jnp.exp(lt - jnp.max(lt, axis=0, keepdims=True))
    at = et / jnp.sum(et, axis=0, keepdims=True)
    for c in range(tm // LANES):
        afft_ref[c] = at[:, c * LANES:(c + 1) * LANES]


def _post(o, w_o, x, g_ffn, w_r):
    n = x.shape[0]
    ko = o.shape[1]
    wr_p = jnp.pad(w_r, ((0, 0), (0, LANES - N_EXPERTS))).astype(BF16)
    wr_t = w_r.T.astype(BF16)
    e = jnp.arange(N_EXPERTS)
    place = jnp.stack([jnp.zeros((LANES, EXT), F32).at[e, 2 + 3 * e + j].set(1.0) for j in range(3)]).astype(BF16)
    args = [o, w_o.astype(BF16), x, g_ffn.reshape(1, -1), wr_p, wr_t, place]
    in_specs = [pl.BlockSpec((TM, ko), lambda i: (i, 0)), _full(args[1].shape),
                pl.BlockSpec((TM, D_MODEL), lambda i: (i, 0))] + [_full(a.shape) for a in args[3:]]
    return pl.pallas_call(
        _post_kernel, grid=(n // TM,), in_specs=in_specs,
        out_specs=[pl.BlockSpec((TM, D_MODEL), lambda i: (i, 0)),
                   pl.BlockSpec((TM, DE), lambda i: (i, 0)),
                   pl.BlockSpec((TM // LANES, N_EXPERTS, LANES), lambda i: (i, 0, 0))],
        out_shape=[jax.ShapeDtypeStruct((n, D_MODEL), F32),
                   jax.ShapeDtypeStruct((n, DE), BF16),
                   jax.ShapeDtypeStruct((n // LANES, N_EXPERTS, LANES), F32)],
        compiler_params=_params(1, 0), name="post")(*args)


def _select_kernel(aff_ref, tri_ref, pos_ref, carry_ref, *, cap):
    nblk = aff_ref.shape[0]
    bits = lax.bitcast_convert_type(aff_ref[...], I32)

    def count(mask):
        c = jnp.sum(mask.astype(F32), axis=0, keepdims=True)
        return jnp.sum(c, axis=2, keepdims=True)

    def search(_, lohi):
        lo, hi = lohi
        mid = lo + ((hi - lo) >> 1)
        ok = count(bits >= mid) >= cap
        return jnp.where(ok, mid, lo), jnp.where(ok, hi, mid)

    lo0 = jnp.zeros((1, N_EXPERTS, 1), I32)
    hi0 = jnp.full((1, N_EXPERTS, 1), 0x7F800001, I32)
    thr, _ = lax.fori_loop(0, 31, search, (lo0, hi0))

    def prefix(mask):
        m = mask.astype(F32)
        loc = _dot(m.astype(BF16).reshape(nblk * N_EXPERTS, LANES), tri_ref[...])
        loc = loc.reshape(nblk, N_EXPERTS, LANES)
        bsum = loc[:, :, LANES - 1:]
        c = bsum
        k = 1
        while k < nblk:
            c = c + jnp.concatenate([jnp.zeros((k, N_EXPERTS, 1), F32), c[:nblk - k]], axis=0)
            k *= 2
        carry = c - bsum
        return carry + loc - m, carry

    gt = bits > thr
    tie = bits == thr
    need = cap - count(gt)
    tie_rank, _ = prefix(tie)
    sel = gt | (tie & (tie_rank < need))
    pos, carry = prefix(sel)
    pos_ref[...] = jnp.where(sel, pos, -1.0).astype(I32)
    carry_ref[...] = jnp.broadcast_to(carry, carry_ref.shape).astype(I32)


def _select(afft, cap):
    tri = (jnp.arange(LANES)[:, None] <= jnp.arange(LANES)[None, :]).astype(BF16)
    return pl.pallas_call(
        functools.partial(_select_kernel, cap=cap), grid=(1,),
        in_specs=[_full(afft.shape), _full(tri.shape)],
        out_specs=[_full(afft.shape), _full(afft.shape)],
        out_shape=[jax.ShapeDtypeStruct(afft.shape, I32), jax.ShapeDtypeStruct(afft.shape, I32)],
        compiler_params=_params(0, 1), name="select")(afft, tri)


def _tile_counts(base_ref, i):
    bases = [base_ref[i * N_EXPERTS + e] for e in range(N_EXPERTS)]
    cnts = [base_ref[(i + 1) * N_EXPERTS + e] - bases[e] for e in range(N_EXPERTS)]
    most = cnts[0]
    for c in cnts[1:]:
        most = jnp.maximum(most, c)
    return bases, cnts, (most + SEG - 1) // SEG


def _scatter_kernel(base_ref, pos_ref, h_ref, xe_ref, acc_ref, fbuf_ref, nflush_ref, sems):
    i = pl.program_id(0)

    def flush_copy(buf, e, dst):
        return pltpu.make_async_copy(fbuf_ref.at[buf], xe_ref.at[e, pl.ds(dst, FLUSH)], sems.at[buf])

    @pl.when(i == 0)
    def _():
        acc_ref[...] = jnp.zeros_like(acc_ref)
        nflush_ref[0] = 0

    bases, cnts, nrounds = _tile_counts(base_ref, i)
    hb = h_ref[...]
    riota = lax.broadcasted_iota(I32, (WIN, TM), 0)

    def one_round(r, carry):
        onehots, done = [], []
        for e in range(N_EXPERTS):
            s0 = bases[e] + jnp.minimum(r * SEG, cnts[e])
            p = jnp.concatenate([pos_ref[c, e:e + 1, :] for c in range(TM // LANES)], axis=1)
            rank = p - bases[e] - r * SEG
            valid = (p >= 0) & (rank >= 0) & (rank < SEG)
            onehots.append(((riota == rank + (s0 & (SUBLANES - 1))) & valid).astype(BF16))
            done.append(s0)
        rows = _dot(jnp.concatenate(onehots, axis=0), hb)
        for e in range(N_EXPERTS):
            s0 = done[e]
            fill = s0 & (FLUSH - 1)
            al = pl.multiple_of(fill - (fill & (SUBLANES - 1)), SUBLANES)
            acc_ref[e, pl.ds(al, WIN), :] += rows[e * WIN:(e + 1) * WIN]
            s1 = bases[e] + jnp.minimum((r + 1) * SEG, cnts[e])

            @pl.when((s1 // FLUSH) > (s0 // FLUSH))
            def _(e=e, s0=s0):
                started = nflush_ref[0]
                buf = started % FLUSH_BUFS

                @pl.when(started >= FLUSH_BUFS)
                def _():
                    flush_copy(buf, e, 0).wait()

                fbuf_ref[buf] = acc_ref[e, 0:FLUSH, :]
                flush_copy(buf, e, pl.multiple_of((s0 // FLUSH) * FLUSH, FLUSH)).start()
                nflush_ref[0] = started + 1
                acc_ref[e, 0:WIN, :] = acc_ref[e, FLUSH:ACC_ROWS, :]
                acc_ref[e, WIN:ACC_ROWS, :] = jnp.zeros((ACC_ROWS - WIN, DE), F32)
        return carry

    lax.fori_loop(0, nrounds, one_round, 0)

    @pl.when(i == pl.num_programs(0) - 1)
    def _():
        for buf in range(FLUSH_BUFS):
            @pl.when(nflush_ref[0] > buf)
            def _(buf=buf):
                flush_copy(buf, 0, 0).wait()


def _scatter(base, pos, hext, cap):
    n = hext.shape[0]
    grid_spec = pltpu.PrefetchScalarGridSpec(
        num_scalar_prefetch=1, grid=(n // TM,),
        in_specs=[pl.BlockSpec((TM // LANES, N_EXPERTS, LANES), lambda i, b: (i, 0, 0)),
                  pl.BlockSpec((TM, DE), lambda i, b: (i, 0))],
        out_specs=pl.BlockSpec(memory_space=pl.ANY),
        scratch_shapes=[pltpu.VMEM((N_EXPERTS, ACC_ROWS, DE), F32),
                        pltpu.VMEM((FLUSH_BUFS, FLUSH, DE), F32),
                        pltpu.SMEM((1,), I32),
                        pltpu.SemaphoreType.DMA((FLUSH_BUFS,))])
    return pl.pallas_call(
        _scatter_kernel, grid_spec=grid_spec,
        out_shape=jax.ShapeDtypeStruct((N_EXPERTS, cap, DE), F32),
        compiler_params=_params(0, 1), name="scatter")(base, pos, hext)


def _ffn_kernel(xe_ref, wg_ref, wu_ref, wd_ref, ye_ref):
    e = pl.program_id(0)
    rows = xe_ref.shape[0]
    x = xe_ref[:, :D_MODEL].astype(BF16)
    ext = xe_ref[:, D_MODEL:]
    lane = lax.broadcasted_iota(I32, (rows, EXT), 1)
    mine = (lane >= 2 + 3 * e) & (lane < 5 + 3 * e)
    gate = jnp.sum(jnp.where(mine, ext, 0.0), axis=-1, keepdims=True)
    h1 = _dot(x, wg_ref[...])
    hid = h1 * jax.nn.sigmoid(h1) * _dot(x, wu_ref[...])
    ye_ref[:, :D_MODEL] = _dot(hid.astype(BF16), wd_ref[...]) * gate
    ye_ref[:, D_MODEL:] = ext


def _ffn(xe, w_g, w_u, w_d):
    _, cap, _ = xe.shape
    rows = min(FFN_ROWS, cap)
    wspec = lambda shape: pl.BlockSpec((None,) + shape, lambda e, c: (e, 0, 0))
    return pl.pallas_call(
        _ffn_kernel, grid=(N_EXPERTS, cap // rows),
        in_specs=[pl.BlockSpec((None, rows, DE), lambda e, c: (e, c, 0)),
                  wspec((D_MODEL, EXPERT_FF)), wspec((D_MODEL, EXPERT_FF)), wspec((EXPERT_FF, D_MODEL))],
        out_specs=pl.BlockSpec((None, rows, DE), lambda e, c: (e, c, 0)),
        out_shape=jax.ShapeDtypeStruct(xe.shape, F32),
        compiler_params=_params(2, 0), name="ffn")(xe, w_g, w_u, w_d)


def _combine_kernel(base_ref, x1_ref, ye_ref, g_ref, out_ref, ybuf_ref, sems, *, cap, final):
    i = pl.program_id(0)
    half = i % 2
    bases, cnts, nrounds = _tile_counts(base_ref, i)
    out_ref[...] = x1_ref[...]
    lane = lax.broadcasted_iota(I32, (1, N_EXPERTS * WIN), 1)
    trow = (i * TM + lax.broadcasted_iota(I32, (TM, 1), 0)).astype(F32)

    def window_start(tile, e, r):
        lo = base_ref[tile * N_EXPERTS + e] + r * SEG
        return pl.multiple_of(jnp.minimum(lo - (lo & (SUBLANES - 1)), cap - WIN), SUBLANES)

    def window_copies(tile, r, buf):
        return [pltpu.make_async_copy(ye_ref.at[e, pl.ds(window_start(tile, e, r), WIN)],
                                      ybuf_ref.at[buf, e], sems.at[buf]) for e in range(N_EXPERTS)]

    @pl.when(i == 0)
    def _():
        for cp in window_copies(0, 0, 0):
            cp.start()

    @pl.when(i + 1 < pl.num_programs(0))
    def _():
        for cp in window_copies(i + 1, 0, 1 - half):
            cp.start()

    def one_round(r, carry):
        copies = window_copies(i, r, half)

        @pl.when(r > 0)
        def _():
            for cp in copies:
                cp.start()

        for cp in copies:
            cp.wait()
        starts = [window_start(i, e, r) for e in range(N_EXPERTS)]
        y = ybuf_ref[half].reshape(N_EXPERTS * WIN, DE)
        ext_t = y[:, D_MODEL:].T
        tid = ext_t[0:1, :] * 256.0 + ext_t[1:2, :]
        slot = jnp.zeros_like(lane)
        lo_v = jnp.zeros_like(lane)
        hi_v = jnp.zeros_like(lane)
        for e in range(N_EXPERTS):
            seg = (lane >= e * WIN) & (lane < (e + 1) * WIN)
            slot = jnp.where(seg, starts[e] + lane - e * WIN, slot)
            lo_v = jnp.where(seg, bases[e] + r * SEG, lo_v)
            hi_v = jnp.where(seg, bases[e] + jnp.minimum((r + 1) * SEG, cnts[e]), hi_v)
        valid = (slot >= lo_v) & (slot < hi_v)
        onehot = ((tid == trow) & valid).astype(BF16)
        out_ref[...] += _dot(onehot, y[:, :D_MODEL].astype(BF16))
        return carry

    lax.fori_loop(0, nrounds, one_round, 0)
    if final:
        out_ref[...] = _rms(out_ref[...], g_ref[...])


def _combine(base, x1, ye, g_final, final):
    n = x1.shape[0]
    cap = ye.shape[1]
    grid_spec = pltpu.PrefetchScalarGridSpec(
        num_scalar_prefetch=1, grid=(n // TM,),
        in_specs=[pl.BlockSpec((TM, D_MODEL), lambda i, b: (i, 0)),
                  pl.BlockSpec(memory_space=pl.ANY),
                  pl.BlockSpec((1, D_MODEL), lambda i, b: (0, 0))],
        out_specs=pl.BlockSpec((TM, D_MODEL), lambda i, b: (i, 0)),
        scratch_shapes=[pltpu.VMEM((2, N_EXPERTS, WIN, DE), F32), pltpu.SemaphoreType.DMA((2,))])
    return pl.pallas_call(
        functools.partial(_combine_kernel, cap=cap, final=final), grid_spec=grid_spec,
        out_shape=jax.ShapeDtypeStruct((n, D_MODEL), F32),
        compiler_params=_params(0, 1), name="combine")(base, x1, ye, g_final.reshape(1, -1))


def _moe(x1, hext, afft, w_g, w_u, w_d, g_final, final):
    n = x1.shape[0]
    cap = EC_CAPACITY_FACTOR * n // N_EXPERTS
    pos, carry = _select(afft, cap)
    per_tile = carry[::TM // LANES, :, 0]
    base = jnp.concatenate([per_tile, jnp.full((1, N_EXPERTS), cap, I32)], axis=0).reshape(-1)
    xe = _scatter(base, pos, hext, cap)
    ye = _ffn(xe, w_g, w_u, w_d)
    return _combine(base, x1, ye, g_final, final)


def _trunk(x, w):
    batch, seq, _ = x.shape
    x0 = x.reshape(batch * seq, D_MODEL)
    q, k, v = _pre_a(x0, seq, w["g_attn_a"][0], w["w_in_a"][0], w["g_q_a"][0], w["w_uq_a"][0],
                     w["g_kv_a"][0], w["w_ukv_a"][0])
    o = _attention(q, k, v, batch, seq, shared_kv=False)
    x1, hext, afft = _post(o, w["w_o_a"][0], x0, w["g_ffn"][0], w["w_router"][0])
    x2 = _moe(x1, hext, afft, w["w_gate"][0], w["w_up"][0], w["w_down"][0], w["g_final"], final=False)
    q, k, v = _pre_b(x2, seq, w["g_attn_b"][0], w["w_qkv_b"][0], w["g_qn_b"][0], w["g_kn_b"][0])
    o = _attention(q, k, v, batch, seq, shared_kv=True)
    x3, hext, afft = _post(o, w["w_o_b"][0], x2, w["g_ffn"][1], w["w_router"][1])
    y = _moe(x3, hext, afft, w["w_gate"][1], w["w_up"][1], w["w_down"][1], w["g_final"], final=True)
    return y.reshape(batch, seq, D_MODEL)


def kernel(x_prompt, x_sample, g_attn_a, w_in_a, g_q_a, w_uq_a, g_kv_a, w_ukv_a, w_o_a, g_attn_b, w_qkv_b, g_qn_b, g_kn_b, w_o_b, g_ffn, w_router, w_gate, w_up, w_down, g_final):
    w = dict(g_attn_a=g_attn_a, w_in_a=w_in_a, g_q_a=g_q_a, w_uq_a=w_uq_a, g_kv_a=g_kv_a,
             w_ukv_a=w_ukv_a, w_o_a=w_o_a, g_attn_b=g_attn_b, w_qkv_b=w_qkv_b, g_qn_b=g_qn_b,
             g_kn_b=g_kn_b, w_o_b=w_o_b, g_ffn=g_ffn, w_router=w_router,
             w_gate=w_gate.astype(BF16), w_up=w_up.astype(BF16), w_down=w_down.astype(BF16),
             g_final=g_final)
    return _trunk(x_prompt, w), _trunk(x_sample, w)
```

```python
import functools

import jax
import jax.numpy as jnp
from jax import lax
from jax.experimental import pallas as pl
from jax.experimental.pallas import tpu as pltpu

D_MODEL = 1024
HA, Q_LORA, KV_LORA, DN, DR, DV = 16, 384, 256, 64, 32, 64
HB, KVB, HD = 16, 8, 128
N_EXPERTS, EXPERT_FF = 16, 1024
EC_CAPACITY_FACTOR = 2
GRID_W = 64
ROPE_THETA = 10000.0
RMS_EPS = 1e-6

LANES = 128
SUBLANES = 8
HEAD_PAD = LANES
EXT = LANES
DE = D_MODEL + EXT
TM = 256
TQ = 512
SEG = 48
WIN = SEG + SUBLANES
FLUSH = 128
ACC_ROWS = FLUSH + WIN
FLUSH_BUFS = 4
LOG2E = 1.4426950408889634
FFN_ROWS = 256
VMEM_LIMIT = 56 * 1024 * 1024

F32 = jnp.float32
BF16 = jnp.bfloat16
I32 = jnp.int32


def _dot(a, b):
    return jnp.dot(a, b, preferred_element_type=F32)


def _dot_nt(a, b):
    return lax.dot_general(a, b, (((1,), (1,)), ((), ())), preferred_element_type=F32)


def _rms(x, g):
    return x * lax.rsqrt(jnp.mean(x * x, axis=-1, keepdims=True) + RMS_EPS) * g


def _params(n_parallel=0, n_arbitrary=1):
    sem = ("parallel",) * n_parallel + ("arbitrary",) * n_arbitrary
    return pltpu.CompilerParams(dimension_semantics=sem, vmem_limit_bytes=VMEM_LIMIT)


def _full(shape):
    zeros = (0,) * len(shape)
    return pl.BlockSpec(shape, lambda *_: zeros)


def _pre_a_kernel(x_ref, g_ref, wcq_ref, wckv_ref, wkr_ref, wkrr_ref, gq_ref, gkv_ref,
                  wuq_ref, wuqr_ref, wuk_ref, wuv_ref, inv_ref, q_ref, k_ref, v_ref,
                  *, tiles_per_seq, scale):
    tm = x_ref.shape[0]
    hb = _rms(x_ref[...], g_ref[...]).astype(BF16)
    cq = _dot(hb, wcq_ref[...])
    ckv = _dot(hb, wckv_ref[...])
    kr = _dot(hb, wkr_ref[...])
    krr = _dot(hb, wkrr_ref[...])
    cqn = _rms(cq, gq_ref[...]).astype(BF16)
    ckvn = _rms(ckv, gkv_ref[...]).astype(BF16)

    s0 = (pl.program_id(0) % tiles_per_seq) * tm
    pos = (s0 + lax.broadcasted_iota(I32, (tm, 1), 0)).astype(F32)
    ang = pos * inv_ref[...]
    cos = jnp.cos(ang)
    sin = jnp.sin(ang)
    cos_h = jnp.concatenate([cos] * HA, axis=1)
    sin_h = jnp.concatenate([sin] * HA, axis=1)

    q = _dot(cqn, wuq_ref[...])
    qr = _dot(cqn, wuqr_ref[...])
    q_ref[...] = ((q * cos_h + qr * sin_h) * scale).astype(BF16)

    k_rope = kr * cos + krr * sin
    k = _dot(ckvn, wuk_ref[...]) + jnp.concatenate([k_rope] * HA, axis=1)
    k_ref[...] = k.astype(BF16)
    v_ref[...] = _dot(ckvn, wuv_ref[...]).astype(BF16)


def _pre_a(x, seq, g_attn, w_in, g_q, w_uq, g_kv, w_ukv):
    n = x.shape[0]
    w_cq = w_in[:, :Q_LORA]
    w_ckv = w_in[:, Q_LORA:Q_LORA + KV_LORA]
    w_kr_raw = w_in[:, Q_LORA + KV_LORA:]
    half = DR // 2

    def rot_half(w):
        return jnp.concatenate([-w[..., half:], w[..., :half]], axis=-1)

    def place_rope(w):
        return jnp.pad(w, ((0, 0), (DN, HEAD_PAD - DN - DR)))

    w_kr = place_rope(w_kr_raw)
    w_krr = place_rope(rot_half(w_kr_raw))
    uq = w_uq.reshape(Q_LORA, HA, DN + DR)
    pad_q = ((0, 0), (0, 0), (0, HEAD_PAD - DN - DR))
    w_uq_p = jnp.pad(uq, pad_q).reshape(Q_LORA, HA * HEAD_PAD)
    uq_rot = jnp.concatenate([jnp.zeros_like(uq[..., :DN]), rot_half(uq[..., DN:])], axis=-1)
    w_uq_r = jnp.pad(uq_rot, pad_q).reshape(Q_LORA, HA * HEAD_PAD)
    ukv = w_ukv.reshape(KV_LORA, HA, DN + DV)
    w_uk = jnp.pad(ukv[..., :DN], ((0, 0), (0, 0), (0, HEAD_PAD - DN))).reshape(KV_LORA, HA * HEAD_PAD)
    w_uv = ukv[..., DN:].reshape(KV_LORA, HA * DV)
    inv = ROPE_THETA ** (-(jnp.arange(half, dtype=F32) * 2.0 / DR))
    inv_l = jnp.pad(jnp.concatenate([inv, inv]), (DN, HEAD_PAD - DN - DR)).reshape(1, HEAD_PAD)

    weights = [w_cq, w_ckv, w_kr, w_krr]
    ups = [w_uq_p, w_uq_r, w_uk, w_uv]
    args = ([x, g_attn.reshape(1, -1)] + [w.astype(BF16) for w in weights]
            + [g_q.reshape(1, -1), g_kv.reshape(1, -1)] + [w.astype(BF16) for w in ups] + [inv_l])
    in_specs = [pl.BlockSpec((TM, D_MODEL), lambda i: (i, 0))] + [_full(a.shape) for a in args[1:]]
    kern = functools.partial(_pre_a_kernel, tiles_per_seq=seq // TM, scale=float((DN + DR) ** -0.5 * LOG2E))
    return pl.pallas_call(
        kern, grid=(n // TM,), in_specs=in_specs,
        out_specs=[pl.BlockSpec((TM, HA * HEAD_PAD), lambda i: (i, 0)),
                   pl.BlockSpec((TM, HA * HEAD_PAD), lambda i: (i, 0)),
                   pl.BlockSpec((TM, HA * DV), lambda i: (i, 0))],
        out_shape=[jax.ShapeDtypeStruct((n, HA * HEAD_PAD), BF16),
                   jax.ShapeDtypeStruct((n, HA * HEAD_PAD), BF16),
                   jax.ShapeDtypeStruct((n, HA * DV), BF16)],
        compiler_params=_params(1, 0), name="pre_a")(*args)


def _pre_b_kernel(x_ref, g_ref, wqkv_ref, gq_ref, gk_ref, inv_ref, q_ref, k_ref, v_ref,
                  *, tiles_per_seq, scale):
    tm = x_ref.shape[0]
    hb = _rms(x_ref[...], g_ref[...]).astype(BF16)
    proj = _dot(hb, wqkv_ref[...])

    s0 = (pl.program_id(0) % tiles_per_seq) * tm
    s = s0 + lax.broadcasted_iota(I32, (tm, HD), 0)
    lane = lax.broadcasted_iota(I32, (tm, HD), 1)
    half = HD // 2
    coord = jnp.where((lane % half) < (half // 2), s // GRID_W, s % GRID_W).astype(F32)
    ang = coord * inv_ref[...]
    cos = jnp.cos(ang)
    sin = jnp.where(lane < half, -jnp.sin(ang), jnp.sin(ang))

    def norm_rope(xh, g):
        xn = _rms(xh, g)
        return xn * cos + pltpu.roll(xn, half, 1) * sin

    for h in range(HB):
        xh = proj[:, h * HD:(h + 1) * HD]
        q_ref[:, h * HD:(h + 1) * HD] = (norm_rope(xh, gq_ref[...]) * scale).astype(BF16)
    for h in range(KVB):
        xh = proj[:, (HB + h) * HD:(HB + h + 1) * HD]
        k_ref[:, h * HD:(h + 1) * HD] = norm_rope(xh, gk_ref[...]).astype(BF16)
    v_ref[...] = proj[:, (HB + KVB) * HD:].astype(BF16)


def _pre_b(x, seq, g_attn, w_qkv, g_qn, g_kn):
    n = x.shape[0]
    quarter = HD // 4
    inv = ROPE_THETA ** (-(jnp.arange(quarter, dtype=F32) * 2.0 / (HD // 2)))
    inv_l = jnp.concatenate([inv] * 4).reshape(1, HD)
    perm = jnp.concatenate([jnp.arange(0, quarter), jnp.arange(2 * quarter, 3 * quarter),
                            jnp.arange(quarter, 2 * quarter), jnp.arange(3 * quarter, HD)])
    qk = w_qkv[:, :(HB + KVB) * HD].reshape(D_MODEL, HB + KVB, HD)[:, :, perm]
    w_perm = jnp.concatenate([qk.reshape(D_MODEL, (HB + KVB) * HD), w_qkv[:, (HB + KVB) * HD:]], axis=1)
    args = [x, g_attn.reshape(1, -1), w_perm.astype(BF16), g_qn[perm].reshape(1, -1),
            g_kn[perm].reshape(1, -1), inv_l]
    in_specs = [pl.BlockSpec((TM, D_MODEL), lambda i: (i, 0))] + [_full(a.shape) for a in args[1:]]
    kern = functools.partial(_pre_b_kernel, tiles_per_seq=seq // TM, scale=float(HD ** -0.5 * LOG2E))
    return pl.pallas_call(
        kern, grid=(n // TM,), in_specs=in_specs,
        out_specs=[pl.BlockSpec((TM, HB * HD), lambda i: (i, 0)),
                   pl.BlockSpec((TM, KVB * HD), lambda i: (i, 0)),
                   pl.BlockSpec((TM, KVB * HD), lambda i: (i, 0))],
        out_shape=[jax.ShapeDtypeStruct((n, HB * HD), BF16),
                   jax.ShapeDtypeStruct((n, KVB * HD), BF16),
                   jax.ShapeDtypeStruct((n, KVB * HD), BF16)],
        compiler_params=_params(1, 0), name="pre_b")(*args)


def _attn_kernel(q_ref, k_ref, v_ref, o_ref, vaug_ref, *, shared_kv):
    seq = v_ref.shape[0]

    @pl.when(pl.program_id(2) == 0)
    def _():
        v = v_ref[...]
        if shared_kv:
            lane = lax.broadcasted_iota(I32, (seq, HEAD_PAD), 1)
            vaug_ref[0, :, :HEAD_PAD] = v
            vaug_ref[0, :, HEAD_PAD:] = jnp.where(lane == 0, 1.0, 0.0).astype(BF16)
        else:
            lane = lax.broadcasted_iota(I32, v.shape, 1)
            vaug_ref[0] = jnp.where(lane < DV, v, jnp.where(lane == DV, 1.0, 0.0).astype(BF16))
            vaug_ref[1] = jnp.where(lane >= DV, v, jnp.where(lane == 0, 1.0, 0.0).astype(BF16))

    outs = []
    for j in range(2):
        q = q_ref[:, j * HEAD_PAD:(j + 1) * HEAD_PAD]
        k = k_ref[...] if shared_kv else k_ref[:, j * HEAD_PAD:(j + 1) * HEAD_PAD]
        s = _dot_nt(q, k)
        p = jnp.exp2(s - jnp.max(s, axis=-1, keepdims=True)).astype(BF16)
        va = vaug_ref.at[0 if shared_kv else j]
        hq = p.shape[0] // 2
        outs.append(jnp.concatenate([_dot(p[:hq], va[...]), _dot(p[hq:], va[...])], axis=0))
    if shared_kv:
        for j in range(2):
            o = outs[j][:, :HEAD_PAD] / outs[j][:, HEAD_PAD:HEAD_PAD + 1]
            o_ref[:, j * HEAD_PAD:(j + 1) * HEAD_PAD] = o.astype(o_ref.dtype)
    else:
        lane = lax.broadcasted_iota(I32, outs[0].shape, 1)
        o = jnp.where(lane < DV, outs[0] / outs[0][:, DV:DV + 1], outs[1] / outs[1][:, 0:1])
        o_ref[...] = o.astype(o_ref.dtype)


def _attention(q, k, v, batch, seq, shared_kv):
    n = q.shape[0]
    n_pairs = q.shape[1] // (2 * HEAD_PAD)
    kw = HEAD_PAD if shared_kv else 2 * HEAD_PAD
    vw = v.shape[1] // n_pairs
    ow = 2 * HEAD_PAD if shared_kv else 2 * DV
    tq = min(TQ, seq)
    qt = seq // tq
    vaug = (1, seq, 2 * HEAD_PAD) if shared_kv else (2, seq, vw)
    return pl.pallas_call(
        functools.partial(_attn_kernel, shared_kv=shared_kv),
        grid=(batch, n_pairs, qt),
        in_specs=[pl.BlockSpec((tq, 2 * HEAD_PAD), lambda b, h, i: (b * qt + i, h)),
                  pl.BlockSpec((seq, kw), lambda b, h, i: (b, h)),
                  pl.BlockSpec((seq, vw), lambda b, h, i: (b, h))],
        out_specs=pl.BlockSpec((tq, ow), lambda b, h, i: (b * qt + i, h)),
        out_shape=jax.ShapeDtypeStruct((n, n_pairs * ow), BF16),
        scratch_shapes=[pltpu.VMEM(vaug, BF16)],
        compiler_params=_params(2, 1), name="attn_gqa" if shared_kv else "attn_mla")(q, k, v)


def _post_kernel(o_ref, wo_ref, x_ref, g_ref, wr_ref, wrt_ref, place_ref,
                 x1_ref, hext_ref, afft_ref):
    tm = x_ref.shape[0]
    x1 = x_ref[...] + _dot(o_ref[...], wo_ref[...])
    x1_ref[...] = x1
    hb = _rms(x1, g_ref[...]).astype(BF16)

    lane = lax.broadcasted_iota(I32, (tm, LANES), 1)
    logits = jnp.where(lane < N_EXPERTS, _dot(hb, wr_ref[...]), -1e30)
    ex = jnp.exp(logits - jnp.max(logits, axis=-1, keepdims=True))
    aff = ex / jnp.sum(ex, axis=-1, keepdims=True)
    a_hi = aff.astype(BF16)
    r1 = aff - a_hi.astype(F32)
    a_mid = r1.astype(BF16)
    a_lo = (r1 - a_mid.astype(F32)).astype(BF16)
    ext = _dot(a_hi, place_ref[0]) + _dot(a_mid, place_ref[1]) + _dot(a_lo, place_ref[2])
    t = pl.program_id(0) * tm + lax.broadcasted_iota(I32, (tm, LANES), 0)
    ext = ext + jnp.where(lane == 0, (t >> 8).astype(F32), 0.0) + jnp.where(lane == 1, (t & 255).astype(F32), 0.0)
    hext_ref[:, :D_MODEL] = hb
    hext_ref[:, D_MODEL:] = ext.astype(BF16)

    lt = _dot_nt(wrt_ref[...], hb)
    et = jnp.exp(lt - jnp.max(lt, axis=0, keepdims=True))
    at = et / jnp.sum(et, axis=0, keepdims=True)
    for c in range(tm // LANES):
        afft_ref[c] = at[:, c * LANES:(c + 1) * LANES]


def _post(o, w_o, x, g_ffn, w_r):
    n = x.shape[0]
    ko = o.shape[1]
    wr_p = jnp.pad(w_r, ((0, 0), (0, LANES - N_EXPERTS))).astype(BF16)
    wr_t = w_r.T.astype(BF16)
    e = jnp.arange(N_EXPERTS)
    place = jnp.stack([jnp.zeros((LANES, EXT), F32).at[e, 2 + 3 * e + j].set(1.0) for j in range(3)]).astype(BF16)
    args = [o, w_o.astype(BF16), x, g_ffn.reshape(1, -1), wr_p, wr_t, place]
    in_specs = [pl.BlockSpec((TM, ko), lambda i: (i, 0)), _full(args[1].shape),
                pl.BlockSpec((TM, D_MODEL), lambda i: (i, 0))] + [_full(a.shape) for a in args[3:]]
    return pl.pallas_call(
        _post_kernel, grid=(n // TM,), in_specs=in_specs,
        out_specs=[pl.BlockSpec((TM, D_MODEL), lambda i: (i, 0)),
                   pl.BlockSpec((TM, DE), lambda i: (i, 0)),
                   pl.BlockSpec((TM // LANES, N_EXPERTS, LANES), lambda i: (i, 0, 0))],
        out_shape=[jax.ShapeDtypeStruct((n, D_MODEL), F32),
                   jax.ShapeDtypeStruct((n, DE), BF16),
                   jax.ShapeDtypeStruct((n // LANES, N_EXPERTS, LANES), F32)],
        compiler_params=_params(1, 0), name="post")(*args)


def _select_kernel(aff_ref, tri_ref, pos_ref, carry_ref, *, cap):
    nblk = aff_ref.shape[0]
    bits = lax.bitcast_convert_type(aff_ref[...], I32)

    def count(mask):
        c = jnp.sum(mask.astype(F32), axis=0, keepdims=True)
        return jnp.sum(c, axis=2, keepdims=True)

    def search(_, lohi):
        lo, hi = lohi
        mid = lo + ((hi - lo) >> 1)
        ok = count(bits >= mid) >= cap
        return jnp.where(ok, mid, lo), jnp.where(ok, hi, mid)

    lo0 = jnp.zeros((1, N_EXPERTS, 1), I32)
    hi0 = jnp.full((1, N_EXPERTS, 1), 0x7F800001, I32)
    thr, _ = lax.fori_loop(0, 31, search, (lo0, hi0))

    def prefix(mask):
        m = mask.astype(F32)
        loc = _dot(m.astype(BF16).reshape(nblk * N_EXPERTS, LANES), tri_ref[...])
        loc = loc.reshape(nblk, N_EXPERTS, LANES)
        bsum = loc[:, :, LANES - 1:]
        c = bsum
        k = 1
        while k < nblk:
            c = c + jnp.concatenate([jnp.zeros((k, N_EXPERTS, 1), F32), c[:nblk - k]], axis=0)
            k *= 2
        carry = c - bsum
        return carry + loc - m, carry

    gt = bits > thr
    tie = bits == thr
    need = cap - count(gt)
    tie_rank, _ = prefix(tie)
    sel = gt | (tie & (tie_rank < need))
    pos, carry = prefix(sel)
    pos_ref[...] = jnp.where(sel, pos, -1.0).astype(I32)
    carry_ref[...] = jnp.broadcast_to(carry, carry_ref.shape).astype(I32)


def _select(afft, cap):
    tri = (jnp.arange(LANES)[:, None] <= jnp.arange(LANES)[None, :]).astype(BF16)
    return pl.pallas_call(
        functools.partial(_select_kernel, cap=cap), grid=(1,),
        in_specs=[_full(afft.shape), _full(tri.shape)],
        out_specs=[_full(afft.shape), _full(afft.shape)],
        out_shape=[jax.ShapeDtypeStruct(afft.shape, I32), jax.ShapeDtypeStruct(afft.shape, I32)],
        compiler_params=_params(0, 1), name="select")(afft, tri)


def _tile_counts(base_ref, i):
    bases = [base_ref[i * N_EXPERTS + e] for e in range(N_EXPERTS)]
    cnts = [base_ref[(i + 1) * N_EXPERTS + e] - bases[e] for e in range(N_EXPERTS)]
    most = cnts[0]
    for c in cnts[1:]:
        most = jnp.maximum(most, c)
    return bases, cnts, (most + SEG - 1) // SEG


def _scatter_kernel(base_ref, pos_ref, h_ref, xe_ref, acc_ref, fbuf_ref, nflush_ref, sems):
    i = pl.program_id(0)

    def flush_copy(buf, e, dst):
        return pltpu.make_async_copy(fbuf_ref.at[buf], xe_ref.at[e, pl.ds(dst, FLUSH)], sems.at[buf])

    @pl.when(i == 0)
    def _():
        acc_ref[...] = jnp.zeros_like(acc_ref)
        nflush_ref[0] = 0

    bases, cnts, nrounds = _tile_counts(base_ref, i)
    hb = h_ref[...]
    riota = lax.broadcasted_iota(I32, (WIN, TM), 0)

    def one_round(r, carry):
        onehots, done = [], []
        for e in range(N_EXPERTS):
            s0 = bases[e] + jnp.minimum(r * SEG, cnts[e])
            p = jnp.concatenate([pos_ref[c, e:e + 1, :] for c in range(TM // LANES)], axis=1)
            rank = p - bases[e] - r * SEG
            valid = (p >= 0) & (rank >= 0) & (rank < SEG)
            onehots.append(((riota == rank + (s0 & (SUBLANES - 1))) & valid).astype(BF16))
            done.append(s0)
        rows = _dot(jnp.concatenate(onehots, axis=0), hb)
        for e in range(N_EXPERTS):
            s0 = done[e]
            fill = s0 & (FLUSH - 1)
            al = pl.multiple_of(fill - (fill & (SUBLANES - 1)), SUBLANES)
            acc_ref[e, pl.ds(al, WIN), :] += rows[e * WIN:(e + 1) * WIN]
            s1 = bases[e] + jnp.minimum((r + 1) * SEG, cnts[e])

            @pl.when((s1 // FLUSH) > (s0 // FLUSH))
            def _(e=e, s0=s0):
                started = nflush_ref[0]
                buf = started % FLUSH_BUFS

                @pl.when(started >= FLUSH_BUFS)
                def _():
                    flush_copy(buf, e, 0).wait()

                fbuf_ref[buf] = acc_ref[e, 0:FLUSH, :]
                flush_copy(buf, e, pl.multiple_of((s0 // FLUSH) * FLUSH, FLUSH)).start()
                nflush_ref[0] = started + 1
                acc_ref[e, 0:WIN, :] = acc_ref[e, FLUSH:ACC_ROWS, :]
                acc_ref[e, WIN:ACC_ROWS, :] = jnp.zeros((ACC_ROWS - WIN, DE), F32)
        return carry

    lax.fori_loop(0, nrounds, one_round, 0)

    @pl.when(i == pl.num_programs(0) - 1)
    def _():
        for buf in range(FLUSH_BUFS):
            @pl.when(nflush_ref[0] > buf)
            def _(buf=buf):
                flush_copy(buf, 0, 0).wait()


def _scatter(base, pos, hext, cap):
    n = hext.shape[0]
    grid_spec = pltpu.PrefetchScalarGridSpec(
        num_scalar_prefetch=1, grid=(n // TM,),
        in_specs=[pl.BlockSpec((TM // LANES, N_EXPERTS, LANES), lambda i, b: (i, 0, 0)),
                  pl.BlockSpec((TM, DE), lambda i, b: (i, 0))],
        out_specs=pl.BlockSpec(memory_space=pl.ANY),
        scratch_shapes=[pltpu.VMEM((N_EXPERTS, ACC_ROWS, DE), F32),
                        pltpu.VMEM((FLUSH_BUFS, FLUSH, DE), F32),
                        pltpu.SMEM((1,), I32),
                        pltpu.SemaphoreType.DMA((FLUSH_BUFS,))])
    return pl.pallas_call(
        _scatter_kernel, grid_spec=grid_spec,
        out_shape=jax.ShapeDtypeStruct((N_EXPERTS, cap, DE), F32),
        compiler_params=_params(0, 1), name="scatter")(base, pos, hext)


def _ffn_kernel(xe_ref, wg_ref, wu_ref, wd_ref, ye_ref):
    e = pl.program_id(0)
    rows = xe_ref.shape[0]
    x = xe_ref[:, :D_MODEL].astype(BF16)
    ext = xe_ref[:, D_MODEL:]
    lane = lax.broadcasted_iota(I32, (rows, EXT), 1)
    mine = (lane >= 2 + 3 * e) & (lane < 5 + 3 * e)
    gate = jnp.sum(jnp.where(mine, ext, 0.0), axis=-1, keepdims=True)
    h1 = _dot(x, wg_ref[...])
    hid = h1 * jax.nn.sigmoid(h1) * _dot(x, wu_ref[...])
    ye_ref[:, :D_MODEL] = _dot(hid.astype(BF16), wd_ref[...]) * gate
    ye_ref[:, D_MODEL:] = ext


def _ffn(xe, w_g, w_u, w_d):
    _, cap, _ = xe.shape
    rows = min(FFN_ROWS, cap)
    wspec = lambda shape: pl.BlockSpec((None,) + shape, lambda e, c: (e, 0, 0))
    return pl.pallas_call(
        _ffn_kernel, grid=(N_EXPERTS, cap // rows),
        in_specs=[pl.BlockSpec((None, rows, DE), lambda e, c: (e, c, 0)),
                  wspec((D_MODEL, EXPERT_FF)), wspec((D_MODEL, EXPERT_FF)), wspec((EXPERT_FF, D_MODEL))],
        out_specs=pl.BlockSpec((None, rows, DE), lambda e, c: (e, c, 0)),
        out_shape=jax.ShapeDtypeStruct(xe.shape, F32),
        compiler_params=_params(2, 0), name="ffn")(xe, w_g, w_u, w_d)


def _combine_kernel(base_ref, x1_ref, ye_ref, g_ref, out_ref, ybuf_ref, sems, *, cap, final):
    i = pl.program_id(0)
    half = i % 2
    bases, cnts, nrounds = _tile_counts(base_ref, i)
    out_ref[...] = x1_ref[...]
    lane = lax.broadcasted_iota(I32, (1, N_EXPERTS * WIN), 1)
    trow = (i * TM + lax.broadcasted_iota(I32, (TM, 1), 0)).astype(F32)

    def window_start(tile, e, r):
        lo = base_ref[tile * N_EXPERTS + e] + r * SEG
        return pl.multiple_of(jnp.minimum(lo - (lo & (SUBLANES - 1)), cap - WIN), SUBLANES)

    def window_copies(tile, r, buf):
        return [pltpu.make_async_copy(ye_ref.at[e, pl.ds(window_start(tile, e, r), WIN)],
                                      ybuf_ref.at[buf, e], sems.at[buf]) for e in range(N_EXPERTS)]

    @pl.when(i == 0)
    def _():
        for cp in window_copies(0, 0, 0):
            cp.start()

    @pl.when(i + 1 < pl.num_programs(0))
    def _():
        for cp in window_copies(i + 1, 0, 1 - half):
            cp.start()

    def one_round(r, carry):
        copies = window_copies(i, r, half)

        @pl.when(r > 0)
        def _():
            for cp in copies:
                cp.start()

        for cp in copies:
            cp.wait()
        starts = [window_start(i, e, r) for e in range(N_EXPERTS)]
        y = ybuf_ref[half].reshape(N_EXPERTS * WIN, DE)
        ext_t = y[:, D_MODEL:].T
        tid = ext_t[0:1, :] * 256.0 + ext_t[1:2, :]
        slot = jnp.zeros_like(lane)
        lo_v = jnp.zeros_like(lane)
        hi_v = jnp.zeros_like(lane)
        for e in range(N_EXPERTS):
            seg = (lane >= e * WIN) & (lane < (e + 1) * WIN)
            slot = jnp.where(seg, starts[e] + lane - e * WIN, slot)
            lo_v = jnp.where(seg, bases[e] + r * SEG, lo_v)
            hi_v = jnp.where(seg, bases[e] + jnp.minimum((r + 1) * SEG, cnts[e]), hi_v)
        valid = (slot >= lo_v) & (slot < hi_v)
        onehot = ((tid == trow) & valid).astype(BF16)
        out_ref[...] += _dot(onehot, y[:, :D_MODEL].astype(BF16))
        return carry

    lax.fori_loop(0, nrounds, one_round, 0)
    if final:
        out_ref[...] = _rms(out_ref[...], g_ref[...])


def _combine(base, x1, ye, g_final, final):
    n = x1.shape[0]
    cap = ye.shape[1]
    grid_spec = pltpu.PrefetchScalarGridSpec(
        num_scalar_prefetch=1, grid=(n // TM,),
        in_specs=[pl.BlockSpec((TM, D_MODEL), lambda i, b: (i, 0)),
                  pl.BlockSpec(memory_space=pl.ANY),
                  pl.BlockSpec((1, D_MODEL), lambda i, b: (0, 0))],
        out_specs=pl.BlockSpec((TM, D_MODEL), lambda i, b: (i, 0)),
        scratch_shapes=[pltpu.VMEM((2, N_EXPERTS, WIN, DE), F32), pltpu.SemaphoreType.DMA((2,))])
    return pl.pallas_call(
        functools.partial(_combine_kernel, cap=cap, final=final), grid_spec=grid_spec,
        out_shape=jax.ShapeDtypeStruct((n, D_MODEL), F32),
        compiler_params=_params(0, 1), name="combine")(base, x1, ye, g_final.reshape(1, -1))


def _moe(x1, hext, afft, w_g, w_u, w_d, g_final, final):
    n = x1.shape[0]
    cap = EC_CAPACITY_FACTOR * n // N_EXPERTS
    pos, carry = _select(afft, cap)
    per_tile = carry[::TM // LANES, :, 0]
    base = jnp.concatenate([per_tile, jnp.full((1, N_EXPERTS), cap, I32)], axis=0).reshape(-1)
    xe = _scatter(base, pos, hext, cap)
    ye = _ffn(xe, w_g, w_u, w_d)
    return _combine(base, x1, ye, g_final, final)


def _trunk(x, w):
    batch, seq, _ = x.shape
    x0 = x.reshape(batch * seq, D_MODEL)
    q, k, v = _pre_a(x0, seq, w["g_attn_a"][0], w["w_in_a"][0], w["g_q_a"][0], w["w_uq_a"][0],
                     w["g_kv_a"][0], w["w_ukv_a"][0])
    o = _attention(q, k, v, batch, seq, shared_kv=False)
    x1, hext, afft = _post(o, w["w_o_a"][0], x0, w["g_ffn"][0], w["w_router"][0])
    x2 = _moe(x1, hext, afft, w["w_gate"][0], w["w_up"][0], w["w_down"][0], w["g_final"], final=False)
    q, k, v = _pre_b(x2, seq, w["g_attn_b"][0], w["w_qkv_b"][0], w["g_qn_b"][0], w["g_kn_b"][0])
    o = _attention(q, k, v, batch, seq, shared_kv=True)
    x3, hext, afft = _post(o, w["w_o_b"][0], x2, w["g_ffn"][1], w["w_router"][1])
    y = _moe(x3, hext, afft, w["w_gate"][1], w["w_up"][1], w["w_down"][1], w["g_final"], final=True)
    return y.reshape(batch, seq, D_MODEL)


def kernel(x_prompt, x_sample, g_attn_a, w_in_a, g_q_a, w_uq_a, g_kv_a, w_ukv_a, w_o_a, g_attn_b, w_qkv_b, g_qn_b, g_kn_b, w_o_b, g_ffn, w_router, w_gate, w_up, w_down, g_final):
    w = dict(g_attn_a=g_attn_a, w_in_a=w_in_a, g_q_a=g_q_a, w_uq_a=w_uq_a, g_kv_a=g_kv_a,
             w_ukv_a=w_ukv_a, w_o_a=w_o_a, g_attn_b=g_attn_b, w_qkv_b=w_qkv_b, g_qn_b=g_qn_b,
             g_kn_b=g_kn_b, w_o_b=w_o_b, g_ffn=g_ffn, w_router=w_router,
             w_gate=w_gate.astype(BF16), w_up=w_up.astype(BF16), w_down=w_down.astype(BF16),
             g_final=g_final)
    return _trunk(x_prompt, w), _trunk(x_sample, w)
```
